```python
import jax, jax.numpy as jnp
from jax import lax
import numpy as np

D_MODEL = 1024
BATCH = 8
SEQ = 4096
DEPTH = 4

GDN_HEADS = 4
GDN_DK = 128
GDN_DV = 128
GDN_CHUNK = 64
FOX_HEADS = 8
FOX_DH = 64
FOX_BLOCK = 128
MLSTM_HEADS = 4
MLSTM_DK = 64
MLSTM_DV = 128
MLSTM_CHUNK = 64
CONV_WIDTH = 4
N_BRANCH = 3
NORM_EPS = 1e-6

GDN_W = GDN_HEADS * GDN_DV
FOX_W = FOX_HEADS * FOX_DH
MLSTM_W = MLSTM_HEADS * MLSTM_DV
GDN_QKV = 2 * GDN_HEADS * GDN_DK + GDN_W
MLSTM_QK = 2 * MLSTM_HEADS * MLSTM_DK

IN_SPLITS = (
    ('a_qkv', GDN_QKV), ('a_beta', GDN_HEADS), ('a_alpha', GDN_HEADS), ('a_z', GDN_W),
    ('b_qkv', 3 * FOX_W), ('b_f', FOX_HEADS), ('b_z', FOX_W),
    ('c_qk', MLSTM_QK), ('c_v', MLSTM_W), ('c_i', MLSTM_HEADS), ('c_f', MLSTM_HEADS),
    ('c_o', MLSTM_W), ('c_z', MLSTM_W),
    ('gate', N_BRANCH * D_MODEL),
)
D_IN = sum(w for _, w in IN_SPLITS)

kernel_name = 'hybrid_gdn_fox_mlstm_trunk'


def split_cols(u):
    out, off = {}, 0
    for name, w in IN_SPLITS:
        out[name] = u[..., off:off + w]
        off += w
    return out


def rms_norm(x, g):
    xf = x.astype(jnp.float32)
    y = xf * lax.rsqrt(jnp.mean(xf * xf, axis=-1, keepdims=True) + NORM_EPS)
    return (y * g.astype(jnp.float32)).astype(x.dtype)


def l2_normalize(x):
    return x * lax.rsqrt(jnp.sum(x * x, axis=-1, keepdims=True) + NORM_EPS)


def causal_conv_silu(u, w):
    y = lax.conv_general_dilated(
        u, w[:, None, :].astype(u.dtype), window_strides=(1,),
        padding=[(CONV_WIDTH - 1, 0)], dimension_numbers=('NWC', 'WIO', 'NWC'),
        feature_group_count=u.shape[-1])
    return jax.nn.silu(y)


def to_chunks(a, size):
    b, t, h = a.shape[:3]
    return a.reshape(b, t // size, size, h, *a.shape[3:]).swapaxes(2, 3)


def from_chunks(a):
    b, n, h, l, d = a.shape
    return a.swapaxes(2, 3).reshape(b, n * l, h, d)


def gated_delta_rule(q, k, v, beta, log_decay):
    f32 = jnp.float32
    bsz, t, h, dk = q.shape
    dv = v.shape[-1]
    L = GDN_CHUNK
    q = l2_normalize(q.astype(f32)) * dk ** -0.5
    k = l2_normalize(k.astype(f32))
    qc, kc, vc = to_chunks(q, L), to_chunks(k, L), to_chunks(v.astype(f32), L)
    bc, gc = to_chunks(beta.astype(f32), L), to_chunks(log_decay.astype(f32), L)
    gcum = jnp.cumsum(gc, axis=-1)
    causal = jnp.tril(jnp.ones((L, L), bool))
    strict = jnp.tril(jnp.ones((L, L), bool), -1)
    decay = jnp.exp(jnp.where(causal, gcum[..., :, None] - gcum[..., None, :], -jnp.inf))
    kb = kc * bc[..., None]
    lower = jnp.where(strict, jnp.einsum('bnhid,bnhjd->bnhij', kb, kc) * decay, 0.0)
    a_mat = lower + jnp.eye(L, dtype=f32)
    u = lax.linalg.triangular_solve(a_mat, vc * bc[..., None], left_side=True, lower=True, unit_diagonal=True)
    w = lax.linalg.triangular_solve(a_mat, kb * jnp.exp(gcum)[..., None], left_side=True, lower=True, unit_diagonal=True)
    qk = jnp.where(causal, jnp.einsum('bnhid,bnhjd->bnhij', qc, kc) * decay, 0.0)
    q_dec = qc * jnp.exp(gcum)[..., None]
    k_dec = kc * jnp.exp(gcum[..., -1:] - gcum)[..., None]
    g_last = jnp.exp(gcum[..., -1])

    def step(S, xs):
        u_i, w_i, qk_i, qd_i, kd_i, gl_i = xs
        v_new = u_i - jnp.einsum('bhlk,bhkv->bhlv', w_i, S)
        o = jnp.einsum('bhlk,bhkv->bhlv', qd_i, S) + jnp.einsum('bhij,bhjv->bhiv', qk_i, v_new)
        S = S * gl_i[..., None, None] + jnp.einsum('bhlk,bhlv->bhkv', kd_i, v_new)
        return S, o

    xs = tuple(jnp.moveaxis(a, 1, 0) for a in (u, w, qk, q_dec, k_dec, g_last))
    S0 = jnp.zeros((bsz, h, dk, dv), f32)
    _, o = lax.scan(step, S0, xs)
    return from_chunks(jnp.moveaxis(o, 0, 1))


def forgetting_attention(q, k, v, log_f):
    f32 = jnp.float32
    bsz, t, h, d = q.shape
    nb = t // FOX_BLOCK
    c = jnp.cumsum(log_f, axis=1).swapaxes(1, 2)
    kf, vf = k.astype(f32), v.astype(f32)
    qb = (q.astype(f32) * d ** -0.5).reshape(bsz, nb, FOX_BLOCK, h, d).swapaxes(0, 1)
    cb = c.reshape(bsz, h, nb, FOX_BLOCK).transpose(2, 0, 1, 3)
    key_pos = jnp.arange(t)

    def one_block(args):
        blk, q_blk, c_blk = args
        s = jnp.einsum('bqhd,bkhd->bhqk', q_blk, kf) + c_blk[..., :, None] - c[:, :, None, :]
        q_pos = blk * FOX_BLOCK + jnp.arange(FOX_BLOCK)
        s = jnp.where(key_pos[None, :] <= q_pos[:, None], s, -jnp.inf)
        p = jax.nn.softmax(s, axis=-1)
        return jnp.einsum('bhqk,bkhd->bqhd', p, vf)

    o = lax.map(one_block, (jnp.arange(nb), qb, cb))
    return o.swapaxes(0, 1).reshape(bsz, t, h, d)


def mlstm_chunkwise(q, k, v, i_pre, f_pre):
    f32 = jnp.float32
    bsz, t, h, dk = q.shape
    dv = v.shape[-1]
    L = MLSTM_CHUNK
    qc = to_chunks(q.astype(f32) * dk ** -0.5, L)
    kc, vc = to_chunks(k.astype(f32), L), to_chunks(v.astype(f32), L)
    ic = to_chunks(i_pre.astype(f32), L)
    b = jnp.cumsum(to_chunks(jax.nn.log_sigmoid(f_pre.astype(f32)), L), axis=-1)
    b_last = b[..., -1]
    a = b_last[..., None] - b + ic
    m_loc = jnp.max(a, axis=-1)
    wgt = jnp.exp(a - m_loc[..., None])
    dC = jnp.einsum('bnhl,bnhlk,bnhlv->bnhkv', wgt, kc, vc)
    dn = jnp.einsum('bnhl,bnhlk->bnhk', wgt, kc)

    def step(carry, xs):
        C, n, m = carry
        bl, ml, dC_i, dn_i = xs
        m_new = jnp.maximum(bl + m, ml)
        s_old = jnp.exp(bl + m - m_new)
        s_loc = jnp.exp(ml - m_new)
        C_new = C * s_old[..., None, None] + dC_i * s_loc[..., None, None]
        n_new = n * s_old[..., None] + dn_i * s_loc[..., None]
        return (C_new, n_new, m_new), (C, n, m)

    init = (jnp.zeros((bsz, h, dk, dv), f32), jnp.zeros((bsz, h, dk), f32), jnp.zeros((bsz, h), f32))
    xs = tuple(jnp.moveaxis(z, 1, 0) for z in (b_last, m_loc, dC, dn))
    _, (C_prev, n_prev, m_prev) = lax.scan(step, init, xs)
    C_prev, n_prev, m_prev = (jnp.moveaxis(z, 0, 1) for z in (C_prev, n_prev, m_prev))
    causal = jnp.tril(jnp.ones((L, L), bool))
    log_d = jnp.where(causal, b[..., :, None] - b[..., None, :] + ic[..., None, :], -jnp.inf)
    inter = b + m_prev[..., None]
    m_row = jnp.maximum(inter, jnp.max(log_d, axis=-1))
    s_inter = jnp.exp(inter - m_row)
    qk = jnp.einsum('bnhik,bnhjk->bnhij', qc, kc) * jnp.exp(log_d - m_row[..., None])
    num = jnp.einsum('bnhij,bnhjv->bnhiv', qk, vc) + s_inter[..., None] * jnp.einsum('bnhik,bnhkv->bnhiv', qc, C_prev)
    den = jnp.sum(qk, axis=-1) + s_inter * jnp.einsum('bnhik,bnhk->bnhi', qc, n_prev)
    hc = num / jnp.maximum(jnp.abs(den), jnp.exp(-m_row))[..., None]
    return from_chunks(hc)


def hybrid_layer(x, norm_g, w_in, conv_a, a_log, dt_bias, norm_a, fox_f_bias, conv_c,
                 mlstm_i_bias, mlstm_f_bias, norm_c, proj_a, proj_b, proj_c, w_out):
    f32 = jnp.float32
    bsz, t, _ = x.shape
    h = rms_norm(x, norm_g)
    u = split_cols(h @ w_in.astype(x.dtype))

    qkv = causal_conv_silu(u['a_qkv'], conv_a)
    qa, ka, va = jnp.split(qkv, [GDN_HEADS * GDN_DK, 2 * GDN_HEADS * GDN_DK], axis=-1)
    beta = jax.nn.sigmoid(u['a_beta'].astype(f32))
    log_decay = -jnp.exp(a_log.astype(f32)) * jax.nn.softplus(u['a_alpha'].astype(f32) + dt_bias.astype(f32))
    oa = gated_delta_rule(qa.reshape(bsz, t, GDN_HEADS, GDN_DK), ka.reshape(bsz, t, GDN_HEADS, GDN_DK),
                          va.reshape(bsz, t, GDN_HEADS, GDN_DV), beta, log_decay)
    ya = rms_norm(oa, norm_a.reshape(GDN_HEADS, GDN_DV)).reshape(bsz, t, GDN_W).astype(x.dtype) * jax.nn.silu(u['a_z'])

    qb, kb, vb = jnp.split(u['b_qkv'], 3, axis=-1)
    log_f = jax.nn.log_sigmoid(u['b_f'].astype(f32) + fox_f_bias.astype(f32))
    ob = forgetting_attention(qb.reshape(bsz, t, FOX_HEADS, FOX_DH), kb.reshape(bsz, t, FOX_HEADS, FOX_DH),
                              vb.reshape(bsz, t, FOX_HEADS, FOX_DH), log_f)
    yb = ob.reshape(bsz, t, FOX_W).astype(x.dtype) * jax.nn.silu(u['b_z'])

    qk_c = causal_conv_silu(u['c_qk'], conv_c)
    qc, kc = jnp.split(qk_c, 2, axis=-1)
    hc = mlstm_chunkwise(qc.reshape(bsz, t, MLSTM_HEADS, MLSTM_DK), kc.reshape(bsz, t, MLSTM_HEADS, MLSTM_DK),
                         u['c_v'].reshape(bsz, t, MLSTM_HEADS, MLSTM_DV),
                         u['c_i'] + mlstm_i_bias.astype(x.dtype), u['c_f'] + mlstm_f_bias.astype(x.dtype))
    hc = rms_norm(hc, norm_c.reshape(MLSTM_HEADS, MLSTM_DV)).reshape(bsz, t, MLSTM_W).astype(x.dtype)
    yc = jax.nn.sigmoid(u['c_o']) * hc * jax.nn.silu(u['c_z'])

    gates = jax.nn.sigmoid(u['gate']).reshape(bsz, t, N_BRANCH, D_MODEL)
    merged = (gates[:, :, 0] * (ya @ proj_a.astype(x.dtype))
              + gates[:, :, 1] * (yb @ proj_b.astype(x.dtype))
              + gates[:, :, 2] * (yc @ proj_c.astype(x.dtype)))
    return x + merged @ w_out.astype(x.dtype)


def setup_inputs(seed: int = 0) -> dict:
    key = jax.random.key(seed)
    ks = jax.random.split(key, 18)
    f32 = jnp.float32

    def normal(k, shape, scale):
        return jax.random.normal(k, shape, f32) * scale

    x = normal(ks[0], (BATCH, SEQ, D_MODEL), 1.0)
    norm_g = 1.0 + normal(ks[1], (DEPTH, D_MODEL), 0.02)
    w_in = normal(ks[2], (DEPTH, D_MODEL, D_IN), D_MODEL ** -0.5)
    conv_a = normal(ks[3], (DEPTH, CONV_WIDTH, GDN_QKV), CONV_WIDTH ** -0.5)
    a_log = jnp.log(jax.random.uniform(ks[4], (DEPTH, GDN_HEADS), f32, 1.0, 16.0))
    dt = jnp.exp(jax.random.uniform(ks[5], (DEPTH, GDN_HEADS), f32, float(np.log(1e-3)), float(np.log(1e-1))))
    dt_bias = dt + jnp.log(-jnp.expm1(-dt))
    norm_a = 1.0 + normal(ks[6], (DEPTH, GDN_W), 0.02)
    fox_f_bias = jax.random.uniform(ks[7], (DEPTH, FOX_HEADS), f32, 2.0, 6.0)
    conv_c = normal(ks[8], (DEPTH, CONV_WIDTH, MLSTM_QK), CONV_WIDTH ** -0.5)
    mlstm_i_bias = normal(ks[9], (DEPTH, MLSTM_HEADS), 0.1)
    mlstm_f_bias = jax.random.uniform(ks[10], (DEPTH, MLSTM_HEADS), f32, 3.0, 6.0)
    norm_c = 1.0 + normal(ks[11], (DEPTH, MLSTM_W), 0.02)
    proj_a = normal(ks[12], (DEPTH, GDN_W, D_MODEL), GDN_W ** -0.5)
    proj_b = normal(ks[13], (DEPTH, FOX_W, D_MODEL), FOX_W ** -0.5)
    proj_c = normal(ks[14], (DEPTH, MLSTM_W, D_MODEL), MLSTM_W ** -0.5)
    w_out = normal(ks[15], (DEPTH, D_MODEL, D_MODEL), D_MODEL ** -0.5)
    final_g = 1.0 + normal(ks[16], (D_MODEL,), 0.02)
    return {'x': x, 'norm_g': norm_g, 'w_in': w_in, 'conv_a': conv_a, 'a_log': a_log,
            'dt_bias': dt_bias, 'norm_a': norm_a, 'fox_f_bias': fox_f_bias, 'conv_c': conv_c,
            'mlstm_i_bias': mlstm_i_bias, 'mlstm_f_bias': mlstm_f_bias, 'norm_c': norm_c,
            'proj_a': proj_a, 'proj_b': proj_b, 'proj_c': proj_c, 'w_out': w_out, 'final_g': final_g}


def reference(x, norm_g, w_in, conv_a, a_log, dt_bias, norm_a, fox_f_bias, conv_c,
              mlstm_i_bias, mlstm_f_bias, norm_c, proj_a, proj_b, proj_c, w_out, final_g):
    for l in range(DEPTH):
        x = hybrid_layer(x, norm_g[l], w_in[l], conv_a[l], a_log[l], dt_bias[l], norm_a[l],
                         fox_f_bias[l], conv_c[l], mlstm_i_bias[l], mlstm_f_bias[l], norm_c[l],
                         proj_a[l], proj_b[l], proj_c[l], w_out[l])
    return rms_norm(x, final_g)
```

```python
import functools

import jax
import jax.numpy as jnp
from jax import lax
from jax.experimental import pallas as pl
from jax.experimental.pallas import tpu as pltpu

F32 = jnp.float32
BF16 = jnp.bfloat16

D_MODEL = 1024
GDN_HEADS, GDN_DK, GDN_DV = 4, 128, 128
FOX_HEADS, FOX_DH = 8, 64
MLSTM_HEADS, MLSTM_DK, MLSTM_DV = 4, 64, 128
CONV_WIDTH = 4
NORM_EPS = 1e-6
CHUNK = 64
BRANCH_W = 512

_IN_SPLITS = (
    ('a_qkv', 1536), ('a_beta', 4), ('a_alpha', 4), ('a_z', 512),
    ('b_qkv', 1536), ('b_f', 8), ('b_z', 512),
    ('c_qk', 512), ('c_v', 512), ('c_i', 4), ('c_f', 4),
    ('c_o', 512), ('c_z', 512), ('gate', 3072),
)
_MAIN_ORDER = ('a_qkv', 'a_z', 'b_qkv', 'b_z', 'c_qk', 'c_v', 'c_o', 'c_z', 'gate')
N_MAIN = 9216
_SMALL_ORDER = ('a_beta', 'a_alpha', 'b_f', 'c_i', 'c_f')
N_SMALL = 128
BETA0, ALPHA0, FOXF0, CI0, CF0 = 0, 4, 8, 16, 20

VMEM_LIMIT = 48 * 1024 * 1024

TM_PROJ = 1024
TN_PROJ = 1536
TB_REC = 256
TQ_FOX = 256


def _silu(x):
    return x * jax.nn.sigmoid(x)


def _dot(a, b):
    return jnp.dot(a, b, preferred_element_type=F32)


def _dot_nt(a, b):
    return lax.dot_general(a, b, (((1,), (1,)), ((), ())), preferred_element_type=F32)


def _dot_tn(a, b):
    return lax.dot_general(a, b, (((0,), (0,)), ((), ())), preferred_element_type=F32)


def _cumsum(x, axis, segment):
    pos = lax.broadcasted_iota(jnp.int32, x.shape, axis) % segment
    shift = 1
    while shift < segment:
        x = x + jnp.where(pos >= shift, pltpu.roll(x, shift, axis), 0.0)
        shift *= 2
    return x


def _causal_conv_silu(raw, w, tail_ref, out_ref):
    tb = raw.shape[0]
    acc = raw * w[3:4, :]
    for k in range(1, CONV_WIDTH):
        acc = acc + pltpu.roll(raw, k, 0) * w[3 - k:4 - k, :]
    head = raw[0:8, :]
    tl = tail_ref[...]
    row = lax.broadcasted_iota(jnp.int32, head.shape, 0)
    acc8 = head * w[3:4, :]
    for k in range(1, CONV_WIDTH):
        shifted = jnp.where(row < k, pltpu.roll(tl, k, 0), pltpu.roll(head, k, 0))
        acc8 = acc8 + shifted * w[3 - k:4 - k, :]
    tail_ref[...] = raw[tb - 8:tb, :]
    out_ref[...] = _silu(acc)
    out_ref[0:8, :] = _silu(acc8)


def _proj_in_kernel(x_ref, g_ref, w_ref, ws_ref, wst_ref, u_ref, gs_ref, gst_ref, h_scr):
    @pl.when(pl.program_id(1) == 0)
    def _():
        x = x_ref[...]
        h = x * lax.rsqrt(jnp.mean(x * x, axis=-1, keepdims=True) + NORM_EPS) * g_ref[...]
        hb = h.astype(BF16)
        h_scr[...] = hb
        gs_ref[...] = _dot(hb, ws_ref[...])
        gst_ref[...] = _dot_nt(wst_ref[...], hb)

    u_ref[...] = _dot(h_scr[...], w_ref[...]).astype(BF16)


def _proj_in(x2, g, w_main, w_small, w_small_t):
    m = x2.shape[0]
    tm = min(TM_PROJ, m)
    return pl.pallas_call(
        _proj_in_kernel,
        grid=(m // tm, N_MAIN // TN_PROJ),
        in_specs=[
            pl.BlockSpec((tm, D_MODEL), lambda i, j: (i, 0)),
            pl.BlockSpec((1, D_MODEL), lambda i, j: (0, 0)),
            pl.BlockSpec((D_MODEL, TN_PROJ), lambda i, j: (0, j)),
            pl.BlockSpec((D_MODEL, N_SMALL), lambda i, j: (0, 0)),
            pl.BlockSpec((N_SMALL, D_MODEL), lambda i, j: (0, 0)),
        ],
        out_specs=[
            pl.BlockSpec((tm, TN_PROJ), lambda i, j: (i, j)),
            pl.BlockSpec((tm, N_SMALL), lambda i, j: (i, 0)),
            pl.BlockSpec((N_SMALL, tm), lambda i, j: (0, i)),
        ],
        out_shape=[
            jax.ShapeDtypeStruct((m, N_MAIN), BF16),
            jax.ShapeDtypeStruct((m, N_SMALL), F32),
            jax.ShapeDtypeStruct((N_SMALL, m), F32),
        ],
        scratch_shapes=[pltpu.VMEM((tm, D_MODEL), BF16)],
        compiler_params=pltpu.CompilerParams(
            dimension_semantics=("parallel", "arbitrary"), vmem_limit_bytes=VMEM_LIMIT),
        name="proj_in",
    )(x2, g, w_main, w_small, w_small_t)


def _gdn_kernel(u_ref, gs_ref, gst_ref, conv_ref, prow_ref, pcol_ref, norm_ref, o_ref,
                s_scr, tail_scr, qkv_scr):
    tb = u_ref.shape[0]
    n_qk = GDN_HEADS * GDN_DK

    @pl.when(pl.program_id(1) == 0)
    def _():
        s_scr[...] = jnp.zeros_like(s_scr)
        tail_scr[...] = jnp.zeros_like(tail_scr)

    _causal_conv_silu(u_ref[:, 0:3 * n_qk].astype(F32), conv_ref[...], tail_scr, qkv_scr)

    prow = prow_ref[...]
    pcol = pcol_ref[...]
    gs = gs_ref[...]
    beta_c = jax.nn.sigmoid(gs)
    g_c = _cumsum(-jnp.exp(prow[0:1, :]) * jax.nn.softplus(gs + prow[1:2, :]), 0, CHUNK)
    g_r = _cumsum(-jnp.exp(pcol[:, 0:1]) * jax.nn.softplus(gst_ref[...] + pcol[:, 1:2]), 1, CHUNK)

    rr = lax.broadcasted_iota(jnp.int32, (CHUNK, CHUNK), 0)
    cc = lax.broadcasted_iota(jnp.int32, (CHUNK, CHUNK), 1)
    causal = cc <= rr
    strict = cc < rr
    eye = jnp.where(cc == rr, 1.0, 0.0).astype(F32)

    for c in range(tb // CHUNK):
        r0 = c * CHUNK
        for h in range(GDN_HEADS):
            q = qkv_scr[r0:r0 + CHUNK, h * GDN_DK:(h + 1) * GDN_DK]
            k = qkv_scr[r0:r0 + CHUNK, n_qk + h * GDN_DK:n_qk + (h + 1) * GDN_DK]
            v = qkv_scr[r0:r0 + CHUNK, 2 * n_qk + h * GDN_DV:2 * n_qk + (h + 1) * GDN_DV]
            q = q * lax.rsqrt(jnp.sum(q * q, axis=-1, keepdims=True) + NORM_EPS) * GDN_DK ** -0.5
            k = k * lax.rsqrt(jnp.sum(k * k, axis=-1, keepdims=True) + NORM_EPS)
            gc = g_c[r0:r0 + CHUNK, ALPHA0 + h:ALPHA0 + h + 1]
            gr = g_r[ALPHA0 + h:ALPHA0 + h + 1, r0:r0 + CHUNK]
            gl = g_c[r0 + CHUNK - 1:r0 + CHUNK, ALPHA0 + h:ALPHA0 + h + 1]
            bt = beta_c[r0:r0 + CHUNK, BETA0 + h:BETA0 + h + 1]

            decay = jnp.exp(jnp.where(causal, gc - gr, -jnp.inf))
            kb = k * bt
            k16 = k.astype(BF16)
            n_mat = jnp.where(strict, _dot_nt(kb.astype(BF16), k16) * decay, 0.0)
            qk = _dot_nt(q.astype(BF16), k16) * decay

            p = -n_mat
            t_inv = eye + p
            for _ in range(5):
                p16 = p.astype(BF16)
                p = _dot(p16, p16)
                t_inv = t_inv + _dot(p.astype(BF16), t_inv.astype(BF16))
            rhs = jnp.concatenate([v * bt, kb * jnp.exp(gc)], axis=1)
            uw = rhs + _dot((t_inv - eye).astype(BF16), rhs.astype(BF16))
            u_i = uw[:, 0:GDN_DV]
            w_i = uw[:, GDN_DV:GDN_DV + GDN_DK]

            s_old = s_scr[h]
            s16 = s_old.astype(BF16)
            v_new = u_i - _dot(w_i.astype(BF16), s16)
            v16 = v_new.astype(BF16)
            o = _dot((q * jnp.exp(gc)).astype(BF16), s16) + _dot(qk.astype(BF16), v16)
            k_dec = k * jnp.exp(gl - gc)
            s_scr[h] = s_old * jnp.exp(gl) + _dot_tn(k_dec.astype(BF16), v16)

            on = o * lax.rsqrt(jnp.mean(o * o, axis=-1, keepdims=True) + NORM_EPS)
            on = on * norm_ref[:, h * GDN_DV:(h + 1) * GDN_DV]
            z = u_ref[r0:r0 + CHUNK, 3 * n_qk + h * GDN_DV:3 * n_qk + (h + 1) * GDN_DV].astype(F32)
            o_ref[r0:r0 + CHUNK, h * GDN_DV:(h + 1) * GDN_DV] = (on * _silu(z)).astype(BF16)


def _gdn(u, gs, gst, conv_a, prow, pcol, norm_a, bsz, t):
    tb = min(TB_REC, t)
    nt = t // tb
    n_qkv = 3 * GDN_HEADS * GDN_DK
    return pl.pallas_call(
        _gdn_kernel,
        grid=(bsz, nt),
        in_specs=[
            pl.BlockSpec((tb, 2048), lambda b, i: (b * nt + i, 0)),
            pl.BlockSpec((tb, N_SMALL), lambda b, i: (b * nt + i, 0)),
            pl.BlockSpec((8, tb), lambda b, i: (0, b * nt + i)),
            pl.BlockSpec((CONV_WIDTH, n_qkv), lambda b, i: (0, 0)),
            pl.BlockSpec((2, N_SMALL), lambda b, i: (0, 0)),
            pl.BlockSpec((8, 2), lambda b, i: (0, 0)),
            pl.BlockSpec((1, BRANCH_W), lambda b, i: (0, 0)),
        ],
        out_specs=pl.BlockSpec((tb, BRANCH_W), lambda b, i: (b * nt + i, 0)),
        out_shape=jax.ShapeDtypeStruct((bsz * t, BRANCH_W), BF16),
        scratch_shapes=[
            pltpu.VMEM((GDN_HEADS, GDN_DK, GDN_DV), F32),
            pltpu.VMEM((8, n_qkv), F32),
            pltpu.VMEM((tb, n_qkv), F32),
        ],
        compiler_params=pltpu.CompilerParams(
            dimension_semantics=("parallel", "arbitrary"), vmem_limit_bytes=VMEM_LIMIT),
        name="gdn",
    )(u, gs, gst, conv_a, prow, pcol, norm_a)


def _fox_kernel(q_ref, k_ref, v_ref, z_ref, gs_ref, gst_ref, brow_ref, bcol_ref, o_ref,
                crow_scr, carry_scr):
    tq = q_ref.shape[0]
    t = k_ref.shape[0]
    qi = pl.program_id(1)

    @pl.when(qi == 0)
    def _():
        crow_scr[...] = _cumsum(jax.nn.log_sigmoid(gst_ref[...] + bcol_ref[...]), 1, t)
        carry_scr[...] = jnp.zeros_like(carry_scr)

    c_col = _cumsum(jax.nn.log_sigmoid(gs_ref[...] + brow_ref[...]), 0, tq) + carry_scr[...]
    carry_scr[...] = c_col[tq - 1:tq, :]

    rr = lax.broadcasted_iota(jnp.int32, (tq, tq), 0)
    cc = lax.broadcasted_iota(jnp.int32, (tq, tq), 1)
    causal = cc <= rr

    for h in range(FOX_HEADS):
        lo, hi = h * FOX_DH, (h + 1) * FOX_DH
        q = q_ref[:, lo:hi] * jnp.asarray(FOX_DH ** -0.5, BF16)
        ct = c_col[:, FOXF0 + h:FOXF0 + h + 1]

        def scores(j):
            r = pl.multiple_of(j * tq, tq)
            k = k_ref[pl.ds(r, tq), lo:hi]
            v = v_ref[pl.ds(r, tq), lo:hi]
            cs = crow_scr[h:h + 1, pl.ds(r, tq)]
            return _dot_nt(q, k) + (ct - cs), v

        def update(s, v, carry):
            m, l, acc = carry
            m_new = jnp.maximum(m, jnp.max(s, axis=-1, keepdims=True))
            alpha = jnp.exp(m - m_new)
            p = jnp.exp(s - m_new)
            l = l * alpha + jnp.sum(p, axis=-1, keepdims=True)
            acc = acc * alpha + _dot(p.astype(BF16), v)
            return m_new, l, acc

        def kv_step(j, carry):
            s, v = scores(j)
            return update(s, v, carry)

        init = (jnp.full((tq, 1), -jnp.inf, F32), jnp.zeros((tq, 1), F32),
                jnp.zeros((tq, FOX_DH), F32))
        carry = lax.fori_loop(0, qi, kv_step, init)
        s, v = scores(qi)
        _, l, acc = update(jnp.where(causal, s, -jnp.inf), v, carry)
        o = acc / l
        o_ref[:, lo:hi] = (o * _silu(z_ref[:, lo:hi].astype(F32))).astype(BF16)


def _fox(u, gs, gst, brow, bcol, bsz, t):
    tq = min(TQ_FOX, t)
    nq = t // tq
    w = FOX_HEADS * FOX_DH
    return pl.pallas_call(
        _fox_kernel,
        grid=(bsz, nq),
        in_specs=[
            pl.BlockSpec((tq, w), lambda b, i: (b * nq + i, 4)),
            pl.BlockSpec((t, w), lambda b, i: (b, 5)),
            pl.BlockSpec((t, w), lambda b, i: (b, 6)),
            pl.BlockSpec((tq, w), lambda b, i: (b * nq + i, 7)),
            pl.BlockSpec((tq, N_SMALL), lambda b, i: (b * nq + i, 0)),
            pl.BlockSpec((8, t), lambda b, i: (1, b)),
            pl.BlockSpec((1, N_SMALL), lambda b, i: (0, 0)),
            pl.BlockSpec((8, 1), lambda b, i: (0, 0)),
        ],
        out_specs=pl.BlockSpec((tq, w), lambda b, i: (b * nq + i, 0)),
        out_shape=jax.ShapeDtypeStruct((bsz * t, w), BF16),
        scratch_shapes=[pltpu.VMEM((8, t), F32), pltpu.VMEM((1, N_SMALL), F32)],
        compiler_params=pltpu.CompilerParams(
            dimension_semantics=("parallel", "arbitrary"), vmem_limit_bytes=VMEM_LIMIT),
        name="fox",
    )(u, u, u, u, gs, gst, brow, bcol)


def _mlstm_kernel(u_ref, gs_ref, gst_ref, conv_ref, brow_ref, bcol_ref, norm_ref, o_ref,
                  c_scr, n_scr, m_scr, tail_scr, qk_scr):
    tb = u_ref.shape[0]
    n_q = MLSTM_HEADS * MLSTM_DK
    v0 = 2 * n_q
    o0 = v0 + MLSTM_HEADS * MLSTM_DV
    z0 = o0 + MLSTM_HEADS * MLSTM_DV

    @pl.when(pl.program_id(1) == 0)
    def _():
        c_scr[...] = jnp.zeros_like(c_scr)
        n_scr[...] = jnp.zeros_like(n_scr)
        m_scr[...] = jnp.zeros_like(m_scr)
        tail_scr[...] = jnp.zeros_like(tail_scr)

    _causal_conv_silu(u_ref[:, 0:2 * n_q].astype(F32), conv_ref[...], tail_scr, qk_scr)

    gs = gs_ref[...] + brow_ref[...]
    b_c = _cumsum(jax.nn.log_sigmoid(gs), 0, CHUNK)
    gt = gst_ref[...] + bcol_ref[...]
    b_r = _cumsum(jax.nn.log_sigmoid(gt), 1, CHUNK)

    rr = lax.broadcasted_iota(jnp.int32, (CHUNK, CHUNK), 0)
    cc = lax.broadcasted_iota(jnp.int32, (CHUNK, CHUNK), 1)
    causal = cc <= rr

    for c in range(tb // CHUNK):
        r0 = c * CHUNK
        for h in range(MLSTM_HEADS):
            q = qk_scr[r0:r0 + CHUNK, h * MLSTM_DK:(h + 1) * MLSTM_DK] * MLSTM_DK ** -0.5
            k = qk_scr[r0:r0 + CHUNK, n_q + h * MLSTM_DK:n_q + (h + 1) * MLSTM_DK]
            v16 = u_ref[r0:r0 + CHUNK, v0 + h * MLSTM_DV:v0 + (h + 1) * MLSTM_DV]
            q16 = q.astype(BF16)
            bc = b_c[r0:r0 + CHUNK, CF0 + h:CF0 + h + 1]
            ic = gs[r0:r0 + CHUNK, CI0 + h:CI0 + h + 1]
            bl = b_c[r0 + CHUNK - 1:r0 + CHUNK, CF0 + h:CF0 + h + 1]
            br = b_r[4 + h:5 + h, r0:r0 + CHUNK]
            ir = gt[h:h + 1, r0:r0 + CHUNK]

            c_prev = c_scr[h]
            n_prev = n_scr[h:h + 1, :]
            m_prev = m_scr[h:h + 1, 0:1]

            a = bl - bc + ic
            m_loc = jnp.max(a, axis=0, keepdims=True)
            kw = k * jnp.exp(a - m_loc)
            d_c = _dot_tn(kw.astype(BF16), v16)
            d_n = jnp.sum(kw, axis=0, keepdims=True)

            log_d = jnp.where(causal, bc - br + ir, -jnp.inf)
            inter = bc + m_prev
            m_row = jnp.maximum(inter, jnp.max(log_d, axis=-1, keepdims=True))
            s_inter = jnp.exp(inter - m_row)
            qk = _dot_nt(q16, k.astype(BF16)) * jnp.exp(log_d - m_row)
            num = _dot(qk.astype(BF16), v16) + s_inter * _dot(q16, c_prev.astype(BF16))
            den = (jnp.sum(qk, axis=-1, keepdims=True)
                   + s_inter * jnp.sum(q * n_prev, axis=-1, keepdims=True))
            hc = num / jnp.maximum(jnp.abs(den), jnp.exp(-m_row))

            m_new = jnp.maximum(bl + m_prev, m_loc)
            sc_old = jnp.exp(bl + m_prev - m_new)
            sc_loc = jnp.exp(m_loc - m_new)
            c_scr[h] = c_prev * sc_old + d_c * sc_loc
            n_scr[h:h + 1, :] = n_prev * sc_old + d_n * sc_loc
            m_scr[h:h + 1, :] = jnp.broadcast_to(m_new, (1, N_SMALL))

            hn = hc * lax.rsqrt(jnp.mean(hc * hc, axis=-1, keepdims=True) + NORM_EPS)
            hn = hn * norm_ref[:, h * MLSTM_DV:(h + 1) * MLSTM_DV]
            og = u_ref[r0:r0 + CHUNK, o0 + h * MLSTM_DV:o0 + (h + 1) * MLSTM_DV].astype(F32)
            z = u_ref[r0:r0 + CHUNK, z0 + h * MLSTM_DV:z0 + (h + 1) * MLSTM_DV].astype(F32)
            o_ref[r0:r0 + CHUNK, h * MLSTM_DV:(h + 1) * MLSTM_DV] = (
                jax.nn.sigmoid(og) * hn * _silu(z)).astype(BF16)


def _mlstm(u, gs, gst, conv_c, brow, bcol, norm_c, bsz, t):
    tb = min(TB_REC, t)
    nt = t // tb
    n_qk = 2 * MLSTM_HEADS * MLSTM_DK
    return pl.pallas_call(
        _mlstm_kernel,
        grid=(bsz, nt),
        in_specs=[
            pl.BlockSpec((tb, 2048), lambda b, i: (b * nt + i, 2)),
            pl.BlockSpec((tb, N_SMALL), lambda b, i: (b * nt + i, 0)),
            pl.BlockSpec((8, tb), lambda b, i: (2, b * nt + i)),
            pl.BlockSpec((CONV_WIDTH, n_qk), lambda b, i: (0, 0)),
            pl.BlockSpec((1, N_SMALL), lambda b, i: (0, 0)),
            pl.BlockSpec((8, 1), lambda b, i: (0, 0)),
            pl.BlockSpec((1, BRANCH_W), lambda b, i: (0, 0)),
        ],
        out_specs=pl.BlockSpec((tb, BRANCH_W), lambda b, i: (b * nt + i, 0)),
        out_shape=jax.ShapeDtypeStruct((bsz * t, BRANCH_W), BF16),
        scratch_shapes=[
            pltpu.VMEM((MLSTM_HEADS, MLSTM_DK, MLSTM_DV), F32),
            pltpu.VMEM((8, MLSTM_DK), F32),
            pltpu.VMEM((8, N_SMALL), F32),
            pltpu.VMEM((8, n_qk), F32),
            pltpu.VMEM((tb, n_qk), F32),
        ],
        compiler_params=pltpu.CompilerParams(
            dimension_semantics=("parallel", "arbitrary"), vmem_limit_bytes=VMEM_LIMIT),
        name="mlstm",
    )(u, gs, gst, conv_c, brow, bcol, norm_c)


def _merge_kernel(x_ref, ya_ref, yb_ref, yc_ref, gate_ref, pa_ref, pb_ref, pc_ref, wo_ref, fg_ref,
                  o_ref, *, final):
    g = jax.nn.sigmoid(gate_ref[...].astype(F32))
    merged = (g[:, 0:D_MODEL] * _dot(ya_ref[...], pa_ref[...])
              + g[:, D_MODEL:2 * D_MODEL] * _dot(yb_ref[...], pb_ref[...])
              + g[:, 2 * D_MODEL:3 * D_MODEL] * _dot(yc_ref[...], pc_ref[...]))
    y = x_ref[...] + _dot(merged.astype(BF16), wo_ref[...])
    if final:
        y = y * lax.rsqrt(jnp.mean(y * y, axis=-1, keepdims=True) + NORM_EPS) * fg_ref[...]
    o_ref[...] = y


def _merge(x2, ya, yb, yc, u, pa, pb, pc, wo, fg, final):
    m = x2.shape[0]
    tm = min(TM_PROJ // 2, m)
    row = lambda i: (i, 0)
    fixed = lambda i: (0, 0)
    return pl.pallas_call(
        functools.partial(_merge_kernel, final=final),
        grid=(m // tm,),
        in_specs=[
            pl.BlockSpec((tm, D_MODEL), row),
            pl.BlockSpec((tm, BRANCH_W), row),
            pl.BlockSpec((tm, BRANCH_W), row),
            pl.BlockSpec((tm, BRANCH_W), row),
            pl.BlockSpec((tm, 3 * D_MODEL), lambda i: (i, 2)),
            pl.BlockSpec((BRANCH_W, D_MODEL), fixed),
            pl.BlockSpec((BRANCH_W, D_MODEL), fixed),
            pl.BlockSpec((BRANCH_W, D_MODEL), fixed),
            pl.BlockSpec((D_MODEL, D_MODEL), fixed),
            pl.BlockSpec((1, D_MODEL), fixed),
        ],
        out_specs=pl.BlockSpec((tm, D_MODEL), row),
        out_shape=jax.ShapeDtypeStruct((m, D_MODEL), F32),
        compiler_params=pltpu.CompilerParams(
            dimension_semantics=("parallel",), vmem_limit_bytes=VMEM_LIMIT),
        name="merge_out",
    )(x2, ya, yb, yc, u, pa, pb, pc, wo, fg)


def _split_in_weights(w_in):
    cols, off = {}, 0
    for name, width in _IN_SPLITS:
        cols[name] = (off, off + width)
        off += width
    main = jnp.concatenate([w_in[..., cols[n][0]:cols[n][1]] for n in _MAIN_ORDER], axis=-1)
    small = jnp.concatenate([w_in[..., cols[n][0]:cols[n][1]] for n in _SMALL_ORDER], axis=-1)
    small = jnp.pad(small, ((0, 0), (0, 0), (0, N_SMALL - small.shape[-1])))
    return main.astype(BF16), small.astype(BF16), jnp.swapaxes(small, 1, 2).astype(BF16)


def _lanes(vec, start):
    depth, n = vec.shape
    return jnp.pad(vec.astype(F32), ((0, 0), (start, N_SMALL - start - n)))[:, None, :]


def kernel(x, norm_g, w_in, conv_a, a_log, dt_bias, norm_a, fox_f_bias, conv_c, mlstm_i_bias,
           mlstm_f_bias, norm_c, proj_a, proj_b, proj_c, w_out, final_g):
    bsz, t, d = x.shape
    depth = w_in.shape[0]
    w_main, w_small, w_small_t = _split_in_weights(w_in)

    gdn_row = jnp.concatenate([_lanes(a_log, ALPHA0), _lanes(dt_bias, ALPHA0)], axis=1)
    gdn_col = jnp.pad(jnp.stack([a_log, dt_bias], axis=-1).astype(F32), ((0, 0), (ALPHA0, 0), (0, 0)))
    fox_row = _lanes(fox_f_bias, FOXF0)
    fox_col = fox_f_bias.astype(F32)[:, :, None]
    ml_row = _lanes(mlstm_i_bias, CI0) + _lanes(mlstm_f_bias, CF0)
    ml_col = jnp.concatenate([mlstm_i_bias, mlstm_f_bias], axis=-1).astype(F32)[:, :, None]

    pa, pb, pc, wo = (w.astype(BF16) for w in (proj_a, proj_b, proj_c, w_out))
    fg = final_g.astype(F32)[None, :]

    x2 = x.reshape(bsz * t, d)
    for l in range(depth):
        u, gs, gst = _proj_in(x2, norm_g[l][None, :], w_main[l], w_small[l], w_small_t[l])
        ya = _gdn(u, gs, gst, conv_a[l], gdn_row[l], gdn_col[l], norm_a[l][None, :], bsz, t)
        yb = _fox(u, gs, gst, fox_row[l], fox_col[l], bsz, t)
        yc = _mlstm(u, gs, gst, conv_c[l], ml_row[l], ml_col[l], norm_c[l][None, :], bsz, t)
        x2 = _merge(x2, ya, yb, yc, u, pa[l], pb[l], pc[l], wo[l], fg, final=(l == depth - 1))
    return x2.reshape(bsz, t, d)
```
